```python
import jax, jax.numpy as jnp
from jax import lax
import numpy as np

D_MODEL = 1024
BATCH = 8
SEQ = 4096
DEPTH = 2

HEAD_DIM = 64
SGU_WIDTH = 3 * D_MODEL // 8
CONV_WIDTH = 3 * D_MODEL // 8
POOL_WIDTH = D_MODEL - SGU_WIDTH - CONV_WIDTH
SGU_HEADS = SGU_WIDTH // HEAD_DIM
CHUNK = 128
CONV_K = 31
POOL_WINDOWS = (2, 4, 8, 16)
POOL_GROUPS = len(POOL_WINDOWS)
POOL_GDIM = POOL_WIDTH // POOL_GROUPS
IN_WIDTH = 2 * SGU_WIDTH + 2 * CONV_WIDTH + POOL_WIDTH
D_FF = ((8 * D_MODEL // 3 + 127) // 128) * 128
FFN_CONV_K = 3
N_MOD = 6
EPS = 1e-6
MOD_INIT_STD = 0.02

kernel_name = "hybrid_sgu_conformer_pool_convffn_block"


def rms_norm(x, g):
    xf = x.astype(jnp.float32)
    y = xf * lax.rsqrt(jnp.mean(xf * xf, axis=-1, keepdims=True) + EPS)
    return (y * g.astype(jnp.float32)).astype(x.dtype)


def layer_norm(x, g, b):
    xf = x.astype(jnp.float32)
    mu = jnp.mean(xf, axis=-1, keepdims=True)
    var = jnp.mean(jnp.square(xf - mu), axis=-1, keepdims=True)
    y = (xf - mu) * lax.rsqrt(var + EPS)
    return (y * g.astype(jnp.float32) + b.astype(jnp.float32)).astype(x.dtype)


def causal_depthwise_conv(x, w, b):
    k, ch = w.shape
    y = lax.conv_general_dilated(
        x, w[:, None, :].astype(x.dtype), window_strides=(1,), padding=[(k - 1, 0)],
        dimension_numbers=("NWC", "WIO", "NWC"), feature_group_count=ch)
    return y + b


def spatial_gating(z, norm_g, norm_b, w_s, b_s):
    bsz, s, _ = z.shape
    u, v = jnp.split(z, 2, axis=-1)
    v = v.reshape(bsz, s // CHUNK, CHUNK, SGU_HEADS, HEAD_DIM)
    v = layer_norm(v, norm_g.reshape(SGU_HEADS, HEAD_DIM), norm_b.reshape(SGU_HEADS, HEAD_DIM))
    mask = jnp.tril(jnp.ones((CHUNK, CHUNK), dtype=bool))
    w = jnp.where(mask[None], w_s, jnp.zeros_like(w_s))
    f = jnp.einsum("hts,bnshd->bnthd", w, v) + b_s.T[None, None, :, :, None]
    return u * f.reshape(bsz, s, SGU_WIDTH)


def conformer_conv(z, conv_w, conv_b, norm_g, norm_b):
    a, g = jnp.split(z, 2, axis=-1)
    h = a * jax.nn.sigmoid(g)
    h = causal_depthwise_conv(h, conv_w, conv_b)
    return jax.nn.silu(layer_norm(h, norm_g, norm_b))


def multiscale_pool(z, pool_w, pool_scale):
    bsz, s, _ = z.shape
    zf = z.astype(jnp.float32)
    cs = jnp.pad(jnp.cumsum(zf, axis=1), ((0, 0), (1, 0), (0, 0)))
    pos1 = jnp.arange(1, s + 1, dtype=jnp.int32)
    means = []
    for gi, win in enumerate(POOL_WINDOWS):
        sl = slice(gi * POOL_GDIM, (gi + 1) * POOL_GDIM)
        hi = cs[:, 1:, sl]
        lo = jnp.pad(cs[:, : s + 1 - win, sl], ((0, 0), (win - 1, 0), (0, 0)))
        count = jnp.minimum(pos1, win).astype(jnp.float32)[None, :, None]
        means.append((hi - lo) / count)
    d = (jnp.concatenate(means, axis=-1) - zf).astype(z.dtype)
    d = d.reshape(bsz, s, POOL_GROUPS, POOL_GDIM)
    y = jnp.einsum("bsgc,gcd->bsgd", d, pool_w).reshape(bsz, s, POOL_WIDTH)
    return y * pool_scale


def setup_inputs(seed: int = 0) -> dict:
    key = jax.random.key(seed)
    ks = jax.random.split(key, 32)
    L, D = DEPTH, D_MODEL
    nrm = lambda k, shape, std: (jax.random.normal(k, shape, jnp.float32) * std)
    gain = lambda k, shape: 1.0 + 0.05 * jax.random.normal(k, shape, jnp.float32)
    return {
        "x": jax.random.normal(ks[0], (BATCH, SEQ, D), jnp.float32),
        "c": jax.random.normal(ks[1], (BATCH, D), jnp.float32),
        "mod_w": nrm(ks[2], (L, D, N_MOD * D), MOD_INIT_STD),
        "mod_b": nrm(ks[3], (L, N_MOD * D), 0.01),
        "mix_pre_g": gain(ks[4], (L, D)),
        "mix_post_g": gain(ks[5], (L, D)),
        "w_in": nrm(ks[6], (L, D, IN_WIDTH), D ** -0.5),
        "sgu_norm_g": gain(ks[7], (L, SGU_WIDTH)),
        "sgu_norm_b": nrm(ks[8], (L, SGU_WIDTH), 0.02),
        "sgu_w": nrm(ks[9], (L, SGU_HEADS, CHUNK, CHUNK), CHUNK ** -0.5),
        "sgu_b": gain(ks[10], (L, SGU_HEADS, CHUNK)),
        "conv_w": nrm(ks[11], (L, CONV_K, CONV_WIDTH), CONV_K ** -0.5),
        "conv_b": nrm(ks[12], (L, CONV_WIDTH), 0.02),
        "conv_norm_g": gain(ks[13], (L, CONV_WIDTH)),
        "conv_norm_b": nrm(ks[14], (L, CONV_WIDTH), 0.02),
        "pool_w": nrm(ks[15], (L, POOL_GROUPS, POOL_GDIM, POOL_GDIM), POOL_GDIM ** -0.5),
        "pool_scale": gain(ks[16], (L, POOL_WIDTH)),
        "branch_g": gain(ks[17], (L, D)),
        "w_out": nrm(ks[18], (L, D, D), D ** -0.5),
        "ffn_pre_g": gain(ks[19], (L, D)),
        "ffn_post_g": gain(ks[20], (L, D)),
        "ffn_up": nrm(ks[21], (L, D, 2 * D_FF), D ** -0.5),
        "ffn_conv_w": nrm(ks[22], (L, FFN_CONV_K, 2 * D_FF), FFN_CONV_K ** -0.5),
        "ffn_conv_b": nrm(ks[23], (L, 2 * D_FF), 0.02),
        "ffn_down": nrm(ks[24], (L, D_FF, D), D_FF ** -0.5),
    }


def reference(x, c, mod_w, mod_b, mix_pre_g, mix_post_g, w_in, sgu_norm_g, sgu_norm_b, sgu_w, sgu_b,
              conv_w, conv_b, conv_norm_g, conv_norm_b, pool_w, pool_scale, branch_g, w_out,
              ffn_pre_g, ffn_post_g, ffn_up, ffn_conv_w, ffn_conv_b, ffn_down):
    sc = jax.nn.silu(c)
    for l in range(DEPTH):
        mod = sc @ mod_w[l] + mod_b[l]
        sh1, sc1, g1, sh2, sc2, g2 = [m[:, None, :] for m in jnp.split(mod, N_MOD, axis=-1)]

        h = rms_norm(x, mix_pre_g[l]) * (1.0 + sc1) + sh1
        z = h @ w_in[l]
        z_a, z_b, z_c = jnp.split(z, [2 * SGU_WIDTH, 2 * SGU_WIDTH + 2 * CONV_WIDTH], axis=-1)
        y_a = spatial_gating(jax.nn.gelu(z_a), sgu_norm_g[l], sgu_norm_b[l], sgu_w[l], sgu_b[l])
        y_b = conformer_conv(z_b, conv_w[l], conv_b[l], conv_norm_g[l], conv_norm_b[l])
        y_c = multiscale_pool(z_c, pool_w[l], pool_scale[l])
        ga, gb, gc = jnp.split(branch_g[l], [SGU_WIDTH, SGU_WIDTH + CONV_WIDTH])
        y = jnp.concatenate([rms_norm(y_a, ga), rms_norm(y_b, gb), rms_norm(y_c, gc)], axis=-1)
        y = y @ w_out[l]
        x = x + g1 * rms_norm(y, mix_post_g[l])

        h = rms_norm(x, ffn_pre_g[l]) * (1.0 + sc2) + sh2
        u = causal_depthwise_conv(h @ ffn_up[l], ffn_conv_w[l], ffn_conv_b[l])
        ug, uv = jnp.split(u, 2, axis=-1)
        y = (jax.nn.gelu(ug) * uv) @ ffn_down[l]
        x = x + g2 * rms_norm(y, ffn_post_g[l])
    return x
```

```python
import functools

import jax
import jax.numpy as jnp
from jax import lax
from jax.experimental import pallas as pl
from jax.experimental.pallas import tpu as pltpu

HEAD_DIM = 64
CHUNK = 128
CONV_K = 31
POOL_WINDOWS = (2, 4, 8, 16)
FFN_CONV_K = 3
N_MOD = 6
EPS = 1e-6

SUBLANES = 8
LANES = 128

SEQ_TILE = 256
FFN_CHUNK = 256
CONV_HALO = 32
POOL_HALO = 16
VMEM_LIMIT_BYTES = 56 * 1024 * 1024

BF16 = jnp.bfloat16
F32 = jnp.float32


def _dot(a, b):
    return jnp.dot(a, b, preferred_element_type=F32)


def _gelu_tanh(x):
    return jax.nn.gelu(x, approximate=True)


def _mean_sq(x):
    return jnp.mean(x * x, axis=-1, keepdims=True)


def _mod_kernel(c_ref, w_ref, b_ref, o_ref):
    sc = jax.nn.silu(c_ref[...])
    o_ref[0] = _dot(sc.astype(BF16), w_ref[0].astype(BF16)) + b_ref[0]


def _modulation(c, mod_w, mod_b):
    depth, d, nd = mod_w.shape
    bsz = c.shape[0]
    n_blk = nd // d
    return pl.pallas_call(
        _mod_kernel,
        grid=(depth, n_blk),
        in_specs=[
            pl.BlockSpec((bsz, d), lambda l, n: (0, 0)),
            pl.BlockSpec((1, d, d), lambda l, n: (l, 0, n)),
            pl.BlockSpec((1, 1, d), lambda l, n: (l, 0, n)),
        ],
        out_specs=pl.BlockSpec((1, bsz, d), lambda l, n: (l, 0, n)),
        out_shape=jax.ShapeDtypeStruct((depth, bsz, nd), F32),
        compiler_params=pltpu.CompilerParams(dimension_semantics=("arbitrary", "arbitrary")),
        name="adaln_modulation",
    )(c, mod_w, mod_b.reshape(depth, 1, nd))


def _shifted_tap_sum(ext, weights, first_row, out_rows):
    groups = {}
    for k, w in enumerate(weights):
        q, s = divmod(first_row + k, SUBLANES)
        groups.setdefault(s, []).append((q, w))
    total = None
    for s, taps in sorted(groups.items()):
        win = out_rows if s == 0 else out_rows + SUBLANES
        acc = None
        for q, w in taps:
            term = ext[q * SUBLANES:q * SUBLANES + win]
            if w is not None:
                term = term * w
            acc = term if acc is None else acc + term
        part = acc if s == 0 else acc[s:s + out_rows]
        total = part if total is None else total + part
    return total


def _mixer_kernel(x_ref, mod_ref, pre_g_ref, post_g_ref, w_in_ref, sgu_ng_ref, sgu_nb_ref,
                  sgu_w_ref, sgu_b_ref, mavg_ref, conv_w_ref, conv_b_ref, conv_ng_ref, conv_nb_ref,
                  pool_w_ref, pool_scale_ref, branch_g_ref, w_out_ref, o_ref,
                  conv_carry, pool_carry, *, sgu_width, conv_width, pool_width):
    t = pl.program_id(1)
    ts = x_ref.shape[1]
    first_tile = t == 0

    x = x_ref[0]
    shift, scale, gate = mod_ref[0:1, :], mod_ref[1:2, :], mod_ref[2:3, :]
    h = x * lax.rsqrt(_mean_sq(x) + EPS) * (pre_g_ref[...] * (1.0 + scale)) + shift
    z = _dot(h.astype(BF16), w_in_ref[...])

    o_a, o_b, o_c = 0, 2 * sgu_width, 2 * sgu_width + 2 * conv_width

    za = _gelu_tanh(z[:, o_a:o_a + 2 * sgu_width])
    u, v = za[:, :sgu_width], za[:, sgu_width:]
    mavg = mavg_ref[...]
    mu = _dot(v.astype(BF16), mavg)
    vc = v - mu
    var = _dot((vc * vc).astype(BF16), mavg)
    vn = vc * lax.rsqrt(var + EPS) * sgu_ng_ref[...] + sgu_nb_ref[...]
    lane = lax.broadcasted_iota(jnp.int32, (CHUNK, LANES), 1)
    low_head = lane < HEAD_DIM
    f_rows = []
    for c in range(ts // CHUNK):
        f_cols = []
        for p in range(sgu_width // LANES):
            vp = vn[c * CHUNK:(c + 1) * CHUNK, p * LANES:(p + 1) * LANES]
            rhs = jnp.concatenate([jnp.where(low_head, vp, 0.0), jnp.where(low_head, 0.0, vp)], axis=0)
            f_cols.append(_dot(sgu_w_ref[p], rhs.astype(BF16)))
        f_rows.append(jnp.concatenate(f_cols, axis=1) + sgu_b_ref[...])
    y_a = u * jnp.concatenate(f_rows, axis=0)

    hb = z[:, o_b:o_b + conv_width] * jax.nn.sigmoid(z[:, o_b + conv_width:o_b + 2 * conv_width])
    prev_b = jnp.where(first_tile, 0.0, conv_carry[...])
    conv_carry[...] = hb[ts - CONV_HALO:, :]
    conv_cols = []
    for j in range(conv_width // LANES):
        cols = slice(j * LANES, (j + 1) * LANES)
        ext = jnp.concatenate([prev_b[:, cols], hb[:, cols]], axis=0)
        taps = [conv_w_ref[k:k + 1, cols] for k in range(CONV_K)]
        conv_cols.append(_shifted_tap_sum(ext, taps, CONV_HALO - (CONV_K - 1), ts))
    hc = jnp.concatenate(conv_cols, axis=1) + conv_b_ref[...]
    mu_b = jnp.mean(hc, axis=-1, keepdims=True)
    hcc = hc - mu_b
    var_b = jnp.mean(hcc * hcc, axis=-1, keepdims=True)
    y_b = jax.nn.silu(hcc * lax.rsqrt(var_b + EPS) * conv_ng_ref[...] + conv_nb_ref[...])

    zc = z[:, o_c:o_c + pool_width]
    prev_c = jnp.where(first_tile, 0.0, pool_carry[...])
    pool_carry[...] = zc[ts - POOL_HALO:, :]
    row1 = (lax.broadcasted_iota(jnp.int32, (ts, LANES), 0) + (t * ts + 1)).astype(F32)
    lane_t = lax.broadcasted_iota(jnp.int32, (ts, LANES), 1)
    d_cols = []
    for j in range(pool_width // LANES):
        cols = slice(j * LANES, (j + 1) * LANES)
        ext = jnp.concatenate([prev_c[:, cols], zc[:, cols]], axis=0)
        win_lo, win_hi = POOL_WINDOWS[2 * j], POOL_WINDOWS[2 * j + 1]
        ones = [None] * win_hi
        s_hi = _shifted_tap_sum(ext, ones, POOL_HALO - (win_hi - 1), ts)
        s_lo = _shifted_tap_sum(ext, ones[:win_lo], POOL_HALO - (win_lo - 1), ts)
        in_lo = lane_t < HEAD_DIM
        total = jnp.where(in_lo, s_lo, s_hi)
        count = jnp.minimum(row1, jnp.where(in_lo, float(win_lo), float(win_hi)))
        d_cols.append(total / count - zc[:, cols])
    dpool = jnp.concatenate(d_cols, axis=1)
    y_c = _dot(dpool.astype(BF16), pool_w_ref[...]) * pool_scale_ref[...]

    bg = branch_g_ref[...]
    o2, o3 = sgu_width, sgu_width + conv_width
    y = jnp.concatenate([
        (y_a * lax.rsqrt(_mean_sq(y_a) + EPS) * bg[:, :o2]).astype(BF16),
        (y_b * lax.rsqrt(_mean_sq(y_b) + EPS) * bg[:, o2:o3]).astype(BF16),
        (y_c * lax.rsqrt(_mean_sq(y_c) + EPS) * bg[:, o3:]).astype(BF16),
    ], axis=1)
    yo = _dot(y, w_out_ref[...])
    o_ref[0] = x + gate * (yo * lax.rsqrt(_mean_sq(yo) + EPS) * post_g_ref[...])


def _full(shape):
    nd = len(shape)
    return pl.BlockSpec(shape, lambda b, t: (0,) * nd)


def _mixer(x, mod, p):
    bsz, seq, d = x.shape
    ts = SEQ_TILE
    sgu_width = p["sgu_ng"].shape[1]
    conv_width = p["conv_b"].shape[1]
    pool_width = p["pool_scale"].shape[1]
    params = [p["pre_g"], p["post_g"], p["w_in"], p["sgu_ng"], p["sgu_nb"], p["sgu_w"], p["sgu_b"],
              p["mavg"], p["conv_w"], p["conv_b"], p["conv_ng"], p["conv_nb"], p["pool_w"],
              p["pool_scale"], p["branch_g"], p["w_out"]]
    kern = functools.partial(_mixer_kernel, sgu_width=sgu_width, conv_width=conv_width,
                             pool_width=pool_width)
    return pl.pallas_call(
        kern,
        grid=(bsz, seq // ts),
        in_specs=[pl.BlockSpec((1, ts, d), lambda b, t: (b, t, 0)),
                  pl.BlockSpec((None, N_MOD, d), lambda b, t: (b, 0, 0))]
                 + [_full(a.shape) for a in params],
        out_specs=pl.BlockSpec((1, ts, d), lambda b, t: (b, t, 0)),
        out_shape=jax.ShapeDtypeStruct(x.shape, x.dtype),
        scratch_shapes=[pltpu.VMEM((CONV_HALO, conv_width), F32),
                        pltpu.VMEM((POOL_HALO, pool_width), F32)],
        compiler_params=pltpu.CompilerParams(dimension_semantics=("arbitrary", "arbitrary"),
                                             vmem_limit_bytes=VMEM_LIMIT_BYTES),
        name="token_mixer",
    )(x, mod, *params)


def _ffn_kernel(x_ref, mod_ref, pre_g_ref, post_g_ref, up_ref, cw_ref, cb_ref, down_ref, o_ref,
                up_carry):
    t = pl.program_id(1)
    ts = x_ref.shape[1]
    n_chunks, _, two_ch = up_ref.shape
    ch = two_ch // 2
    first_tile = t == 0

    x = x_ref[0]
    shift, scale, gate = mod_ref[3:4, :], mod_ref[4:5, :], mod_ref[5:6, :]
    h = x * lax.rsqrt(_mean_sq(x) + EPS) * (pre_g_ref[...] * (1.0 + scale)) + shift
    hb = h.astype(BF16)

    acc = jnp.zeros((ts, x.shape[1]), F32)
    for c in range(n_chunks):
        up = _dot(hb, up_ref[c])
        prev = jnp.where(first_tile, 0.0, up_carry[c])
        up_carry[c] = up[ts - SUBLANES:, :]
        ext = jnp.concatenate([prev, up], axis=0)
        cw = cw_ref[c]
        conv = cb_ref[c]
        for k in range(FFN_CONV_K):
            r0 = SUBLANES - (FFN_CONV_K - 1) + k
            conv = conv + ext[r0:r0 + ts] * cw[k:k + 1, :]
        act = _gelu_tanh(conv[:, :ch]) * conv[:, ch:]
        acc = acc + _dot(act.astype(BF16), down_ref[c])
    o_ref[0] = x + gate * (acc * lax.rsqrt(_mean_sq(acc) + EPS) * post_g_ref[...])


def _ffn(x, mod, p):
    bsz, seq, d = x.shape
    ts = SEQ_TILE
    n_chunks, _, two_ch = p["up"].shape
    params = [p["pre_g"], p["post_g"], p["up"], p["cw"], p["cb"], p["down"]]
    return pl.pallas_call(
        _ffn_kernel,
        grid=(bsz, seq // ts),
        in_specs=[pl.BlockSpec((1, ts, d), lambda b, t: (b, t, 0)),
                  pl.BlockSpec((None, N_MOD, d), lambda b, t: (b, 0, 0))]
                 + [_full(a.shape) for a in params],
        out_specs=pl.BlockSpec((1, ts, d), lambda b, t: (b, t, 0)),
        out_shape=jax.ShapeDtypeStruct(x.shape, x.dtype),
        scratch_shapes=[pltpu.VMEM((n_chunks, SUBLANES, two_ch), F32)],
        compiler_params=pltpu.CompilerParams(dimension_semantics=("arbitrary", "arbitrary"),
                                             vmem_limit_bytes=VMEM_LIMIT_BYTES),
        name="conv_ffn",
    )(x, mod, *params)


def _row(a):
    return a.reshape(1, -1)


def _mixer_params(l, mix_pre_g, mix_post_g, w_in, sgu_norm_g, sgu_norm_b, sgu_w, sgu_b, conv_w, conv_b,
                  conv_norm_g, conv_norm_b, pool_w, pool_scale, branch_g, w_out):
    heads, chunk, _ = sgu_w[l].shape
    sgu_width = heads * HEAD_DIM
    w_tri = jnp.where(jnp.tril(jnp.ones((chunk, chunk), dtype=bool))[None], sgu_w[l], 0.0)
    w_pair = w_tri.reshape(heads // 2, 2, chunk, chunk).transpose(0, 2, 1, 3).reshape(heads // 2, chunk, 2 * chunk)
    bias = jnp.repeat(sgu_b[l].T, HEAD_DIM, axis=1)
    head_id = jnp.arange(sgu_width) // HEAD_DIM
    mavg = jnp.where(head_id[:, None] == head_id[None, :], 1.0 / HEAD_DIM, 0.0)
    groups, gdim, _ = pool_w[l].shape
    pool_bd = jnp.zeros((groups * gdim, groups * gdim), F32)
    for g in range(groups):
        pool_bd = pool_bd.at[g * gdim:(g + 1) * gdim, g * gdim:(g + 1) * gdim].set(pool_w[l, g])
    return {
        "pre_g": _row(mix_pre_g[l]), "post_g": _row(mix_post_g[l]), "w_in": w_in[l].astype(BF16),
        "sgu_ng": _row(sgu_norm_g[l]), "sgu_nb": _row(sgu_norm_b[l]), "sgu_w": w_pair.astype(BF16),
        "sgu_b": bias, "mavg": mavg.astype(BF16), "conv_w": conv_w[l], "conv_b": _row(conv_b[l]),
        "conv_ng": _row(conv_norm_g[l]), "conv_nb": _row(conv_norm_b[l]), "pool_w": pool_bd.astype(BF16),
        "pool_scale": _row(pool_scale[l]), "branch_g": _row(branch_g[l]), "w_out": w_out[l].astype(BF16),
    }


def _ffn_params(l, ffn_pre_g, ffn_post_g, ffn_up, ffn_conv_w, ffn_conv_b, ffn_down):
    d, two_ff = ffn_up[l].shape
    d_ff = two_ff // 2
    n_chunks = d_ff // FFN_CHUNK

    def pair_chunks(a):
        lead = a.shape[:-1]
        a = a.reshape(lead + (2, n_chunks, FFN_CHUNK))
        a = jnp.moveaxis(a, -2, 0)
        return a.reshape((n_chunks,) + lead + (2 * FFN_CHUNK,))

    return {
        "pre_g": _row(ffn_pre_g[l]), "post_g": _row(ffn_post_g[l]),
        "up": pair_chunks(ffn_up[l]).astype(BF16),
        "cw": pair_chunks(ffn_conv_w[l]),
        "cb": pair_chunks(ffn_conv_b[l].reshape(1, two_ff)),
        "down": ffn_down[l].reshape(n_chunks, FFN_CHUNK, d).astype(BF16),
    }


def kernel(x, c, mod_w, mod_b, mix_pre_g, mix_post_g, w_in, sgu_norm_g, sgu_norm_b, sgu_w, sgu_b, conv_w, conv_b, conv_norm_g, conv_norm_b, pool_w, pool_scale, branch_g, w_out, ffn_pre_g, ffn_post_g, ffn_up, ffn_conv_w, ffn_conv_b, ffn_down):
    depth = mod_w.shape[0]
    bsz, seq, d = x.shape
    assert seq % SEQ_TILE == 0 and SEQ_TILE % CHUNK == 0 and SEQ_TILE >= CONV_HALO
    assert (ffn_up.shape[2] // 2) % FFN_CHUNK == 0
    mod = _modulation(c, mod_w, mod_b).reshape(depth, bsz, N_MOD, d)
    for l in range(depth):
        mp = _mixer_params(l, mix_pre_g, mix_post_g, w_in, sgu_norm_g, sgu_norm_b, sgu_w, sgu_b, conv_w,
                           conv_b, conv_norm_g, conv_norm_b, pool_w, pool_scale, branch_g, w_out)
        fp = _ffn_params(l, ffn_pre_g, ffn_post_g, ffn_up, ffn_conv_w, ffn_conv_b, ffn_down)
        x = _mixer(x, mod[l], mp)
        x = _ffn(x, mod[l], fp)
    return x
```

```python
import functools

import jax
import jax.numpy as jnp
from jax import lax
from jax.experimental import pallas as pl
from jax.experimental.pallas import tpu as pltpu

HEAD_DIM = 64
CHUNK = 128
CONV_K = 31
POOL_WINDOWS = (2, 4, 8, 16)
FFN_CONV_K = 3
N_MOD = 6
EPS = 1e-6

SUBLANES = 8
LANES = 128

SEQ_TILE = 256
FFN_CHUNK = 256
CONV_HALO = 32
POOL_HALO = 16
ROW_STRIDE = 2
FFN_TAP_BUFS = 3
VMEM_LIMIT_BYTES = 56 * 1024 * 1024

BF16 = jnp.bfloat16
F32 = jnp.float32


def _dot(a, b):
    return jnp.dot(a, b, preferred_element_type=F32)


def _gelu_tanh(x):
    return jax.nn.gelu(x, approximate=True)


def _mean_sq(x):
    return jnp.mean(x * x, axis=-1, keepdims=True)


def _mod_kernel(c_ref, w_ref, b_ref, o_ref):
    sc = jax.nn.silu(c_ref[...])
    o_ref[0] = _dot(sc.astype(BF16), w_ref[0].astype(BF16)) + b_ref[0]


def _modulation(c, mod_w, mod_b):
    depth, d, nd = mod_w.shape
    bsz = c.shape[0]
    n_blk = nd // d
    return pl.pallas_call(
        _mod_kernel,
        grid=(depth, n_blk),
        in_specs=[
            pl.BlockSpec((bsz, d), lambda l, n: (0, 0)),
            pl.BlockSpec((1, d, d), lambda l, n: (l, 0, n)),
            pl.BlockSpec((1, 1, d), lambda l, n: (l, 0, n)),
        ],
        out_specs=pl.BlockSpec((1, bsz, d), lambda l, n: (l, 0, n)),
        out_shape=jax.ShapeDtypeStruct((depth, bsz, nd), F32),
        compiler_params=pltpu.CompilerParams(dimension_semantics=("arbitrary", "arbitrary")),
        name="adaln_modulation",
    )(c, mod_w, mod_b.reshape(depth, 1, nd))


def _shifted_tap_sum(ext, weights, first_row, out_rows):
    groups = {}
    for k, w in enumerate(weights):
        q, s = divmod(first_row + k, SUBLANES)
        groups.setdefault(s, []).append((q, w))
    total = None
    for s, taps in sorted(groups.items()):
        win = out_rows if s == 0 else out_rows + SUBLANES
        acc = None
        for q, w in taps:
            term = ext[q * SUBLANES:q * SUBLANES + win]
            if w is not None:
                term = term * w
            acc = term if acc is None else acc + term
        part = acc if s == 0 else acc[s:s + out_rows]
        total = part if total is None else total + part
    return total


def _mixer_kernel(x_ref, mod_ref, pre_g_ref, post_g_ref, w_in_ref, sgu_ng_ref, sgu_nb_ref,
                  sgu_w_ref, sgu_b_ref, mavg_ref, conv_w_ref, conv_b_ref, conv_ng_ref, conv_nb_ref,
                  pool_w_ref, pool_scale_ref, branch_g_ref, w_out_ref, o_ref,
                  conv_carry, pool_carry, *, sgu_width, conv_width, pool_width):
    t = pl.program_id(1)
    ts = x_ref.shape[1]
    first_tile = t == 0

    x = x_ref[0]
    shift, scale, gate = mod_ref[0:1, :], mod_ref[1:2, :], mod_ref[2:3, :]
    h = x * lax.rsqrt(_mean_sq(x) + EPS) * (pre_g_ref[...] * (1.0 + scale)) + shift
    z = _dot(h.astype(BF16), w_in_ref[...])

    o_a, o_b, o_c = 0, 2 * sgu_width, 2 * sgu_width + 2 * conv_width

    za = _gelu_tanh(z[:, o_a:o_a + 2 * sgu_width])
    u, v = za[:, :sgu_width], za[:, sgu_width:]
    mavg = mavg_ref[...]
    mu = _dot(v.astype(BF16), mavg)
    vc = v - mu
    var = _dot((vc * vc).astype(BF16), mavg)
    vn = vc * lax.rsqrt(var + EPS) * sgu_ng_ref[...] + sgu_nb_ref[...]
    lane = lax.broadcasted_iota(jnp.int32, (CHUNK, LANES), 1)
    low_head = lane < HEAD_DIM
    f_rows = []
    for c in range(ts // CHUNK):
        f_cols = []
        for p in range(sgu_width // LANES):
            vp = vn[c * CHUNK:(c + 1) * CHUNK, p * LANES:(p + 1) * LANES]
            rhs = jnp.concatenate([jnp.where(low_head, vp, 0.0), jnp.where(low_head, 0.0, vp)], axis=0)
            f_cols.append(_dot(sgu_w_ref[p], rhs.astype(BF16)))
        f_rows.append(jnp.concatenate(f_cols, axis=1) + sgu_b_ref[...])
    y_a = u * jnp.concatenate(f_rows, axis=0)

    hb = z[:, o_b:o_b + conv_width] * jax.nn.sigmoid(z[:, o_b + conv_width:o_b + 2 * conv_width])
    prev_b = jnp.where(first_tile, 0.0, conv_carry[...])
    conv_carry[...] = hb[ts - CONV_HALO:, :]
    conv_cols = []
    for j in range(conv_width // LANES):
        cols = slice(j * LANES, (j + 1) * LANES)
        ext = jnp.concatenate([prev_b[:, cols], hb[:, cols]], axis=0)
        taps = [conv_w_ref[k:k + 1, cols] for k in range(CONV_K)]
        conv_cols.append(_shifted_tap_sum(ext, taps, CONV_HALO - (CONV_K - 1), ts))
    hc = jnp.concatenate(conv_cols, axis=1) + conv_b_ref[...]
    mu_b = jnp.mean(hc, axis=-1, keepdims=True)
    hcc = hc - mu_b
    var_b = jnp.mean(hcc * hcc, axis=-1, keepdims=True)
    y_b = jax.nn.silu(hcc * lax.rsqrt(var_b + EPS) * conv_ng_ref[...] + conv_nb_ref[...])

    zc = z[:, o_c:o_c + pool_width]
    prev_c = jnp.where(first_tile, 0.0, pool_carry[...])
    pool_carry[...] = zc[ts - POOL_HALO:, :]
    row1 = (lax.broadcasted_iota(jnp.int32, (ts, LANES), 0) + (t * ts + 1)).astype(F32)
    lane_t = lax.broadcasted_iota(jnp.int32, (ts, LANES), 1)
    d_cols = []
    for j in range(pool_width // LANES):
        cols = slice(j * LANES, (j + 1) * LANES)
        ext = jnp.concatenate([prev_c[:, cols], zc[:, cols]], axis=0)
        win_lo, win_hi = POOL_WINDOWS[2 * j], POOL_WINDOWS[2 * j + 1]
        ones = [None] * win_hi
        s_hi = _shifted_tap_sum(ext, ones, POOL_HALO - (win_hi - 1), ts)
        s_lo = _shifted_tap_sum(ext, ones[:win_lo], POOL_HALO - (win_lo - 1), ts)
        in_lo = lane_t < HEAD_DIM
        total = jnp.where(in_lo, s_lo, s_hi)
        count = jnp.minimum(row1, jnp.where(in_lo, float(win_lo), float(win_hi)))
        d_cols.append(total / count - zc[:, cols])
    dpool = jnp.concatenate(d_cols, axis=1)
    y_c = _dot(dpool.astype(BF16), pool_w_ref[...]) * pool_scale_ref[...]

    bg = branch_g_ref[...]
    o2, o3 = sgu_width, sgu_width + conv_width
    y = jnp.concatenate([
        (y_a * lax.rsqrt(_mean_sq(y_a) + EPS) * bg[:, :o2]).astype(BF16),
        (y_b * lax.rsqrt(_mean_sq(y_b) + EPS) * bg[:, o2:o3]).astype(BF16),
        (y_c * lax.rsqrt(_mean_sq(y_c) + EPS) * bg[:, o3:]).astype(BF16),
    ], axis=1)
    yo = _dot(y, w_out_ref[...])
    o_ref[0] = x + gate * (yo * lax.rsqrt(_mean_sq(yo) + EPS) * post_g_ref[...])


def _full(shape):
    nd = len(shape)
    return pl.BlockSpec(shape, lambda b, t: (0,) * nd)


def _mixer(x, mod, p):
    bsz, seq, d = x.shape
    ts = SEQ_TILE
    sgu_width = p["sgu_ng"].shape[1]
    conv_width = p["conv_b"].shape[1]
    pool_width = p["pool_scale"].shape[1]
    params = [p["pre_g"], p["post_g"], p["w_in"], p["sgu_ng"], p["sgu_nb"], p["sgu_w"], p["sgu_b"],
              p["mavg"], p["conv_w"], p["conv_b"], p["conv_ng"], p["conv_nb"], p["pool_w"],
              p["pool_scale"], p["branch_g"], p["w_out"]]
    kern = functools.partial(_mixer_kernel, sgu_width=sgu_width, conv_width=conv_width,
                             pool_width=pool_width)
    return pl.pallas_call(
        kern,
        grid=(bsz, seq // ts),
        in_specs=[pl.BlockSpec((1, ts, d), lambda b, t: (b, t, 0)),
                  pl.BlockSpec((None, N_MOD, d), lambda b, t: (b, 0, 0))]
                 + [_full(a.shape) for a in params],
        out_specs=pl.BlockSpec((1, ts, d), lambda b, t: (b, t, 0)),
        out_shape=jax.ShapeDtypeStruct(x.shape, x.dtype),
        scratch_shapes=[pltpu.VMEM((CONV_HALO, conv_width), F32),
                        pltpu.VMEM((POOL_HALO, pool_width), F32)],
        compiler_params=pltpu.CompilerParams(dimension_semantics=("arbitrary", "arbitrary"),
                                             vmem_limit_bytes=VMEM_LIMIT_BYTES),
        name="token_mixer",
    )(x, mod, *params)


def _ffn_kernel(x_ref, mod_ref, pre_g_ref, post_g_ref, up_ref, cw_ref, cb_ref, down_ref, o_ref,
                up_carry, *tap_bufs):
    t = pl.program_id(1)
    ts = x_ref.shape[1]
    n_chunks, _, two_ch = up_ref.shape
    ch = two_ch // 2
    n_blk = two_ch // LANES
    first_tile = t == 0

    x = x_ref[0]
    shift, scale, gate = mod_ref[3:4, :], mod_ref[4:5, :], mod_ref[5:6, :]
    h = x * lax.rsqrt(_mean_sq(x) + EPS) * (pre_g_ref[...] * (1.0 + scale)) + shift
    hb = h.astype(BF16)

    def rows(logical_start, n):
        return pl.ds(ROW_STRIDE * (logical_start + SUBLANES), n, stride=ROW_STRIDE)

    def project_up(c):
        buf = tap_bufs[c % len(tap_bufs)]
        up = _dot(hb, up_ref[c])
        for j in range(n_blk):
            buf[j, rows(-SUBLANES, SUBLANES), :] = jnp.where(first_tile, 0.0, up_carry[c, j])
            buf[j, rows(0, ts), :] = up[:, j * LANES:(j + 1) * LANES]
            up_carry[c, j] = up[ts - SUBLANES:, j * LANES:(j + 1) * LANES]

    def gated_act(c):
        buf = tap_bufs[c % len(tap_bufs)]
        cw = cw_ref[c]
        cb = cb_ref[c]
        cols = []
        for j in range(n_blk):
            lanes = slice(j * LANES, (j + 1) * LANES)
            conv = cb[:, lanes]
            for k in range(FFN_CONV_K):
                conv = conv + buf[j, rows(k - (FFN_CONV_K - 1), ts), :] * cw[k:k + 1, lanes]
            cols.append(conv)
        half = n_blk // 2
        gate_u = jnp.concatenate(cols[:half], axis=1)
        val_u = jnp.concatenate(cols[half:], axis=1)
        return (_gelu_tanh(gate_u) * val_u).astype(BF16)

    acc = jnp.zeros((ts, x.shape[1]), F32)
    project_up(0)
    for c in range(n_chunks):
        if c + 1 < n_chunks:
            project_up(c + 1)
        acc = acc + _dot(gated_act(c), down_ref[c])
    o_ref[0] = x + gate * (acc * lax.rsqrt(_mean_sq(acc) + EPS) * post_g_ref[...])


def _ffn(x, mod, p):
    bsz, seq, d = x.shape
    ts = SEQ_TILE
    n_chunks, _, two_ch = p["up"].shape
    params = [p["pre_g"], p["post_g"], p["up"], p["cw"], p["cb"], p["down"]]
    return pl.pallas_call(
        _ffn_kernel,
        grid=(bsz, seq // ts),
        in_specs=[pl.BlockSpec((1, ts, d), lambda b, t: (b, t, 0)),
                  pl.BlockSpec((None, N_MOD, d), lambda b, t: (b, 0, 0))]
                 + [_full(a.shape) for a in params],
        out_specs=pl.BlockSpec((1, ts, d), lambda b, t: (b, t, 0)),
        out_shape=jax.ShapeDtypeStruct(x.shape, x.dtype),
        scratch_shapes=[pltpu.VMEM((n_chunks, two_ch // LANES, SUBLANES, LANES), F32)]
                       + [pltpu.VMEM((two_ch // LANES, ROW_STRIDE * (ts + SUBLANES), LANES), F32)
                          for _ in range(FFN_TAP_BUFS)],
        compiler_params=pltpu.CompilerParams(dimension_semantics=("arbitrary", "arbitrary"),
                                             vmem_limit_bytes=VMEM_LIMIT_BYTES),
        name="conv_ffn",
    )(x, mod, *params)


def _row(a):
    return a.reshape(1, -1)


def _mixer_params(l, mix_pre_g, mix_post_g, w_in, sgu_norm_g, sgu_norm_b, sgu_w, sgu_b, conv_w, conv_b,
                  conv_norm_g, conv_norm_b, pool_w, pool_scale, branch_g, w_out):
    heads, chunk, _ = sgu_w[l].shape
    sgu_width = heads * HEAD_DIM
    w_tri = jnp.where(jnp.tril(jnp.ones((chunk, chunk), dtype=bool))[None], sgu_w[l], 0.0)
    w_pair = w_tri.reshape(heads // 2, 2, chunk, chunk).transpose(0, 2, 1, 3).reshape(heads // 2, chunk, 2 * chunk)
    bias = jnp.repeat(sgu_b[l].T, HEAD_DIM, axis=1)
    head_id = jnp.arange(sgu_width) // HEAD_DIM
    mavg = jnp.where(head_id[:, None] == head_id[None, :], 1.0 / HEAD_DIM, 0.0)
    groups, gdim, _ = pool_w[l].shape
    pool_bd = jnp.zeros((groups * gdim, groups * gdim), F32)
    for g in range(groups):
        pool_bd = pool_bd.at[g * gdim:(g + 1) * gdim, g * gdim:(g + 1) * gdim].set(pool_w[l, g])
    return {
        "pre_g": _row(mix_pre_g[l]), "post_g": _row(mix_post_g[l]), "w_in": w_in[l].astype(BF16),
        "sgu_ng": _row(sgu_norm_g[l]), "sgu_nb": _row(sgu_norm_b[l]), "sgu_w": w_pair.astype(BF16),
        "sgu_b": bias, "mavg": mavg.astype(BF16), "conv_w": conv_w[l], "conv_b": _row(conv_b[l]),
        "conv_ng": _row(conv_norm_g[l]), "conv_nb": _row(conv_norm_b[l]), "pool_w": pool_bd.astype(BF16),
        "pool_scale": _row(pool_scale[l]), "branch_g": _row(branch_g[l]), "w_out": w_out[l].astype(BF16),
    }


def _ffn_params(l, ffn_pre_g, ffn_post_g, ffn_up, ffn_conv_w, ffn_conv_b, ffn_down):
    d, two_ff = ffn_up[l].shape
    d_ff = two_ff // 2
    n_chunks = d_ff // FFN_CHUNK

    def pair_chunks(a):
        lead = a.shape[:-1]
        a = a.reshape(lead + (2, n_chunks, FFN_CHUNK))
        a = jnp.moveaxis(a, -2, 0)
        return a.reshape((n_chunks,) + lead + (2 * FFN_CHUNK,))

    return {
        "pre_g": _row(ffn_pre_g[l]), "post_g": _row(ffn_post_g[l]),
        "up": pair_chunks(ffn_up[l]).astype(BF16),
        "cw": pair_chunks(ffn_conv_w[l]),
        "cb": pair_chunks(ffn_conv_b[l].reshape(1, two_ff)),
        "down": ffn_down[l].reshape(n_chunks, FFN_CHUNK, d).astype(BF16),
    }


def kernel(x, c, mod_w, mod_b, mix_pre_g, mix_post_g, w_in, sgu_norm_g, sgu_norm_b, sgu_w, sgu_b, conv_w, conv_b, conv_norm_g, conv_norm_b, pool_w, pool_scale, branch_g, w_out, ffn_pre_g, ffn_post_g, ffn_up, ffn_conv_w, ffn_conv_b, ffn_down):
    depth = mod_w.shape[0]
    bsz, seq, d = x.shape
    assert seq % SEQ_TILE == 0 and SEQ_TILE % CHUNK == 0 and SEQ_TILE >= CONV_HALO
    assert (ffn_up.shape[2] // 2) % FFN_CHUNK == 0
    mod = _modulation(c, mod_w, mod_b).reshape(depth, bsz, N_MOD, d)
    for l in range(depth):
        mp = _mixer_params(l, mix_pre_g, mix_post_g, w_in, sgu_norm_g, sgu_norm_b, sgu_w, sgu_b, conv_w,
                           conv_b, conv_norm_g, conv_norm_b, pool_w, pool_scale, branch_g, w_out)
        fp = _ffn_params(l, ffn_pre_g, ffn_post_g, ffn_up, ffn_conv_w, ffn_conv_b, ffn_down)
        x = _mixer(x, mod[l], mp)
        x = _ffn(x, mod[l], fp)
    return x
```

```python
import functools
import math

import jax
import jax.numpy as jnp
from jax import lax
from jax.experimental import pallas as pl
from jax.experimental.pallas import tpu as pltpu

HEAD_DIM = 64
CHUNK = 128
CONV_K = 31
POOL_WINDOWS = (2, 4, 8, 16)
FFN_CONV_K = 3
N_MOD = 6
EPS = 1e-6
GELU_C = math.sqrt(2.0 / math.pi)
GELU_A = 0.044715

SUBLANES = 8
LANES = 128

SEQ_TILE = 512
MIXER_SUB_TILE = 256
FFN_SUB_TILE = 256
MIXER_STAGGER = 2
FFN_STAGGER = 2
FFN_CHUNK = 256
FFN_DOWN_GROUP = 4
CONV_HALO = 32
POOL_HALO = 16
ROW_STRIDE = 2
VMEM_LIMIT_BYTES = 56 * 1024 * 1024

BF16 = jnp.bfloat16
F32 = jnp.float32


def _dot(a, b):
    return jnp.dot(a, b, preferred_element_type=F32)


def _gelu_tanh(x):
    half = 0.5 * x
    return half + half * jnp.tanh(x * (GELU_C + (GELU_C * GELU_A) * (x * x)))


def _mean_sq(x):
    return jnp.mean(x * x, axis=-1, keepdims=True)


def _tap_rows(halo, logical_start, n):
    return pl.ds(ROW_STRIDE * (logical_start + halo), n, stride=ROW_STRIDE)


def _run_staggered(stage_lists, lag):
    n = max(len(s) + k * lag for k, s in enumerate(stage_lists))
    for i in range(n):
        for k, stages in enumerate(stage_lists):
            if 0 <= i - k * lag < len(stages):
                stages[i - k * lag]()


def _resident(shape):
    nd = len(shape)
    return pl.BlockSpec(shape, lambda b, t: (0,) * nd, pipeline_mode=pl.Buffered(1))


def _mod_kernel(c_ref, w_ref, b_ref, o_ref):
    sc = jax.nn.silu(c_ref[...])
    o_ref[0] = _dot(sc.astype(BF16), w_ref[0].astype(BF16)) + b_ref[0]


def _modulation(c, mod_w, mod_b):
    depth, d, nd = mod_w.shape
    bsz = c.shape[0]
    n_blk = nd // d
    return pl.pallas_call(
        _mod_kernel,
        grid=(depth, n_blk),
        in_specs=[
            pl.BlockSpec((bsz, d), lambda l, n: (0, 0)),
            pl.BlockSpec((1, d, d), lambda l, n: (l, 0, n)),
            pl.BlockSpec((1, 1, d), lambda l, n: (l, 0, n)),
        ],
        out_specs=pl.BlockSpec((1, bsz, d), lambda l, n: (l, 0, n)),
        out_shape=jax.ShapeDtypeStruct((depth, bsz, nd), F32),
        compiler_params=pltpu.CompilerParams(dimension_semantics=("arbitrary", "arbitrary")),
        name="adaln_modulation",
    )(c, mod_w, mod_b.reshape(depth, 1, nd))


MIXER_PARAMS = ("pre_g", "post_g", "w_in", "sgu_ng", "sgu_nb", "sgu_w", "sgu_b", "mavg", "conv_w",
                "conv_b", "conv_ng", "conv_nb", "pool_w", "pool_scale", "branch_g", "w_out")


def _mixer_stages(p, x_ref, mod_ref, o_ref, conv_buf, pool_buf, r0, n, pos0, first_tile):
    ts = x_ref.shape[1]
    sgu_width = p["sgu_ng"].shape[1]
    conv_width = p["conv_b"].shape[1]
    pool_width = p["pool_scale"].shape[1]
    o_a, o_b, o_c = 0, 2 * sgu_width, 2 * sgu_width + 2 * conv_width
    o2, o3 = sgu_width, sgu_width + conv_width
    s = {}

    def project_in():
        if r0 == 0:
            for buf, halo in ((conv_buf, CONV_HALO), (pool_buf, POOL_HALO)):
                for j in range(buf.shape[0]):
                    prev_tail = buf[j, _tap_rows(halo, ts - halo, halo), :]
                    buf[j, _tap_rows(halo, -halo, halo), :] = jnp.where(first_tile, 0.0, prev_tail)
        x = x_ref[0, r0:r0 + n, :]
        in_gain = p["pre_g"][...] * (1.0 + mod_ref[1:2, :])
        h16 = (x * lax.rsqrt(_mean_sq(x) + EPS) * in_gain + mod_ref[0:1, :]).astype(BF16)
        s["z_b"] = _dot(h16, p["w_in"][:, o_b:o_c])
        s["z_a"] = _dot(h16, p["w_in"][:, o_a:o_b])
        s["zc"] = _dot(h16, p["w_in"][:, o_c:o_c + pool_width])

    def fill_tap_buffers():
        z_b = s.pop("z_b")
        hb = z_b[:, :conv_width] * jax.nn.sigmoid(z_b[:, conv_width:])
        for j in range(conv_width // LANES):
            conv_buf[j, _tap_rows(CONV_HALO, r0, n), :] = hb[:, j * LANES:(j + 1) * LANES]
        for j in range(pool_width // LANES):
            pool_buf[j, _tap_rows(POOL_HALO, r0, n), :] = s["zc"][:, j * LANES:(j + 1) * LANES]

    def sgu_mean():
        za = _gelu_tanh(s.pop("z_a"))
        s["u"], s["v"] = za[:, :sgu_width], za[:, sgu_width:]
        s["mu"] = _dot(s["v"].astype(BF16), p["mavg"][...])

    def sgu_var():
        s["vc"] = s.pop("v") - s.pop("mu")
        s["var"] = _dot((s["vc"] * s["vc"]).astype(BF16), p["mavg"][...])

    def sgu_spatial():
        low_head = lax.broadcasted_iota(jnp.int32, (CHUNK, LANES), 1) < HEAD_DIM
        vn = s.pop("vc") * lax.rsqrt(s.pop("var") + EPS) * p["sgu_ng"][...] + p["sgu_nb"][...]
        f_rows = []
        for c in range(n // CHUNK):
            f_cols = []
            for q in range(sgu_width // LANES):
                vp = vn[c * CHUNK:(c + 1) * CHUNK, q * LANES:(q + 1) * LANES]
                rhs = jnp.concatenate([jnp.where(low_head, vp, 0.0), jnp.where(low_head, 0.0, vp)], axis=0)
                f_cols.append(_dot(p["sgu_w"][q], rhs.astype(BF16)))
            f_rows.append(jnp.concatenate(f_cols, axis=1) + p["sgu_b"][...])
        y_a = s.pop("u") * jnp.concatenate(f_rows, axis=0)
        s["n_a"] = (y_a * lax.rsqrt(_mean_sq(y_a) + EPS) * p["branch_g"][:, :o2]).astype(BF16)

    def conv_branch():
        n_groups = n // SUBLANES
        conv_cols = []
        for j in range(conv_width // LANES):
            cols = slice(j * LANES, (j + 1) * LANES)
            taps = [jnp.broadcast_to(p["conv_w"][k:k + 1, cols], (SUBLANES, LANES)) for k in range(CONV_K)]
            acc = [jnp.broadcast_to(p["conv_b"][:, cols], (SUBLANES, LANES))] * n_groups
            for start in range(-(CONV_K - 1), n - SUBLANES + 1):
                win = conv_buf[j, _tap_rows(CONV_HALO, r0 + start, SUBLANES), :]
                for g in range(n_groups):
                    k = start - SUBLANES * g + (CONV_K - 1)
                    if 0 <= k < CONV_K:
                        acc[g] = acc[g] + win * taps[k]
            conv_cols.append(jnp.concatenate(acc, axis=0))
        hc = jnp.concatenate(conv_cols, axis=1)
        hcc = hc - jnp.mean(hc, axis=-1, keepdims=True)
        var_b = jnp.mean(hcc * hcc, axis=-1, keepdims=True)
        y_b = jax.nn.silu(hcc * lax.rsqrt(var_b + EPS) * p["conv_ng"][...] + p["conv_nb"][...])
        s["n_b"] = (y_b * lax.rsqrt(_mean_sq(y_b) + EPS) * p["branch_g"][:, o2:o3]).astype(BF16)

    def pool_branch():
        zc = s.pop("zc")
        row1 = (lax.broadcasted_iota(jnp.int32, (n, LANES), 0) + (pos0 + r0 + 1)).astype(F32)
        in_lo = lax.broadcasted_iota(jnp.int32, (n, LANES), 1) < HEAD_DIM
        d_cols = []
        for j in range(pool_width // LANES):
            cols = slice(j * LANES, (j + 1) * LANES)
            win_lo, win_hi = POOL_WINDOWS[2 * j], POOL_WINDOWS[2 * j + 1]
            s_lo = zc[:, cols]
            for back in range(1, win_lo):
                s_lo = s_lo + pool_buf[j, _tap_rows(POOL_HALO, r0 - back, n), :]
            s_hi = s_lo
            for back in range(win_lo, win_hi):
                s_hi = s_hi + pool_buf[j, _tap_rows(POOL_HALO, r0 - back, n), :]
            total = jnp.where(in_lo, s_lo, s_hi)
            count = jnp.minimum(row1, jnp.where(in_lo, float(win_lo), float(win_hi)))
            d_cols.append(total / count - zc[:, cols])
        dpool = jnp.concatenate(d_cols, axis=1)
        y_c = _dot(dpool.astype(BF16), p["pool_w"][...]) * p["pool_scale"][...]
        s["n_c"] = (y_c * lax.rsqrt(_mean_sq(y_c) + EPS) * p["branch_g"][:, o3:]).astype(BF16)

    def project_out():
        s["yo"] = (_dot(s.pop("n_a"), p["w_out"][:o2, :]) + _dot(s.pop("n_b"), p["w_out"][o2:o3, :])
                   + _dot(s.pop("n_c"), p["w_out"][o3:, :]))

    def residual():
        yo = s.pop("yo")
        out_gain = mod_ref[2:3, :] * p["post_g"][...]
        o_ref[0, r0:r0 + n, :] = x_ref[0, r0:r0 + n, :] + yo * lax.rsqrt(_mean_sq(yo) + EPS) * out_gain

    return [project_in, fill_tap_buffers, sgu_mean, sgu_var, sgu_spatial, conv_branch, pool_branch,
            project_out, residual]


def _mixer_kernel(x_ref, mod_ref, *refs):
    p = dict(zip(MIXER_PARAMS, refs))
    o_ref, conv_buf, pool_buf = refs[len(MIXER_PARAMS):]
    t = pl.program_id(1)
    ts = x_ref.shape[1]
    _run_staggered([_mixer_stages(p, x_ref, mod_ref, o_ref, conv_buf, pool_buf, r0, MIXER_SUB_TILE,
                                  t * ts, t == 0)
                    for r0 in range(0, ts, MIXER_SUB_TILE)], MIXER_STAGGER)


def _mixer(x, mod, p):
    bsz, seq, d = x.shape
    ts = SEQ_TILE
    conv_width = p["conv_b"].shape[1]
    pool_width = p["pool_scale"].shape[1]
    params = [p[k] for k in MIXER_PARAMS]
    return pl.pallas_call(
        _mixer_kernel,
        grid=(bsz, seq // ts),
        in_specs=[pl.BlockSpec((1, ts, d), lambda b, t: (b, t, 0)),
                  pl.BlockSpec((None, N_MOD, d), lambda b, t: (b, 0, 0))]
                 + [_resident(a.shape) for a in params],
        out_specs=pl.BlockSpec((1, ts, d), lambda b, t: (b, t, 0)),
        out_shape=jax.ShapeDtypeStruct(x.shape, x.dtype),
        scratch_shapes=[pltpu.VMEM((conv_width // LANES, ROW_STRIDE * (CONV_HALO + ts), LANES), F32),
                        pltpu.VMEM((pool_width // LANES, ROW_STRIDE * (POOL_HALO + ts), LANES), F32)],
        compiler_params=pltpu.CompilerParams(dimension_semantics=("arbitrary", "arbitrary"),
                                             vmem_limit_bytes=VMEM_LIMIT_BYTES),
        name="token_mixer",
    )(x, mod, *params)


FFN_PARAMS = ("pre_g", "post_g", "up", "cw", "cb", "down")


def _ffn_stages(p, x_ref, mod_ref, o_ref, up_carry, tap_bufs, r0, n, first_tile):
    ts = x_ref.shape[1]
    d_ff = p["down"].shape[0]
    ch = FFN_CHUNK
    n_chunks = d_ff // ch
    n_blk = 2 * ch // LANES
    half = n_blk // 2
    first_sub, last_sub = r0 == 0, r0 + n == ts
    s = {}

    def block_cols(c, j):
        start = c * ch + (j % half) * LANES + (d_ff if j >= half else 0)
        return slice(start, start + LANES)

    def prologue():
        x = x_ref[0, r0:r0 + n, :]
        in_gain = p["pre_g"][...] * (1.0 + mod_ref[4:5, :])
        s["h16"] = (x * lax.rsqrt(_mean_sq(x) + EPS) * in_gain + mod_ref[3:4, :]).astype(BF16)

    def project_up(c):
        buf = tap_bufs[c % len(tap_bufs)]
        up = jnp.concatenate([_dot(s["h16"], p["up"][:, c * ch:(c + 1) * ch]),
                              _dot(s["h16"], p["up"][:, d_ff + c * ch:d_ff + (c + 1) * ch])], axis=1)
        for j in range(n_blk):
            lanes = slice(j * LANES, (j + 1) * LANES)
            if first_sub:
                buf[j, _tap_rows(SUBLANES, -SUBLANES, SUBLANES), :] = jnp.where(
                    first_tile, 0.0, up_carry[c, j])
            buf[j, _tap_rows(SUBLANES, r0, n), :] = up[:, lanes]
            if last_sub:
                up_carry[c, j] = up[n - SUBLANES:, lanes]

    def gated_act(c):
        buf = tap_bufs[c % len(tap_bufs)]
        cols = []
        for j in range(n_blk):
            src = block_cols(c, j)
            conv = p["cb"][:, src]
            for k in range(FFN_CONV_K):
                tap = buf[j, _tap_rows(SUBLANES, r0 + k - (FFN_CONV_K - 1), n), :]
                conv = conv + tap * p["cw"][k:k + 1, src]
            cols.append(conv)
        gate_u = jnp.concatenate(cols[:half], axis=1)
        val_u = jnp.concatenate(cols[half:], axis=1)
        return (_gelu_tanh(gate_u) * val_u).astype(BF16)

    def chunk(c):
        if c + 1 < n_chunks:
            project_up(c + 1)
        s.setdefault("acts", []).append(gated_act(c))
        if len(s["acts"]) == FFN_DOWN_GROUP or c + 1 == n_chunks:
            acts = s.pop("acts")
            k0, k1 = (c + 1 - len(acts)) * ch, (c + 1) * ch
            part = _dot(jnp.concatenate(acts, axis=1), p["down"][k0:k1, :])
            s["acc"] = s["acc"] + part if "acc" in s else part

    def epilogue():
        acc = s.pop("acc")
        out_gain = mod_ref[5:6, :] * p["post_g"][...]
        o_ref[0, r0:r0 + n, :] = x_ref[0, r0:r0 + n, :] + acc * lax.rsqrt(_mean_sq(acc) + EPS) * out_gain

    return ([prologue, functools.partial(project_up, 0)]
            + [functools.partial(chunk, c) for c in range(n_chunks)] + [epilogue])


def _ffn_kernel(x_ref, mod_ref, *refs):
    p = dict(zip(FFN_PARAMS, refs))
    o_ref, up_carry = refs[len(FFN_PARAMS):len(FFN_PARAMS) + 2]
    tap_bufs = refs[len(FFN_PARAMS) + 2:]
    first_tile = pl.program_id(1) == 0
    _run_staggered([_ffn_stages(p, x_ref, mod_ref, o_ref, up_carry, tap_bufs, r0, FFN_SUB_TILE, first_tile)
                    for r0 in range(0, x_ref.shape[1], FFN_SUB_TILE)], FFN_STAGGER)


def _ffn(x, mod, p):
    bsz, seq, d = x.shape
    ts = SEQ_TILE
    n_chunks = p["down"].shape[0] // FFN_CHUNK
    n_blk = 2 * FFN_CHUNK // LANES
    n_sub = ts // FFN_SUB_TILE
    n_tap_bufs = FFN_STAGGER * (n_sub - 1) + 2
    params = [p[k] for k in FFN_PARAMS]
    return pl.pallas_call(
        _ffn_kernel,
        grid=(bsz, seq // ts),
        in_specs=[pl.BlockSpec((1, ts, d), lambda b, t: (b, t, 0)),
                  pl.BlockSpec((None, N_MOD, d), lambda b, t: (b, 0, 0))]
                 + [_resident(a.shape) for a in params],
        out_specs=pl.BlockSpec((1, ts, d), lambda b, t: (b, t, 0)),
        out_shape=jax.ShapeDtypeStruct(x.shape, x.dtype),
        scratch_shapes=[pltpu.VMEM((n_chunks, n_blk, SUBLANES, LANES), F32)]
                       + [pltpu.VMEM((n_blk, ROW_STRIDE * (SUBLANES + ts), LANES), F32)
                          for _ in range(n_tap_bufs)],
        compiler_params=pltpu.CompilerParams(dimension_semantics=("arbitrary", "arbitrary"),
                                             vmem_limit_bytes=VMEM_LIMIT_BYTES),
        name="conv_ffn",
    )(x, mod, *params)


def _row(a):
    return a.reshape(1, -1)


def _mixer_params(l, mix_pre_g, mix_post_g, w_in, sgu_norm_g, sgu_norm_b, sgu_w, sgu_b, conv_w, conv_b,
                  conv_norm_g, conv_norm_b, pool_w, pool_scale, branch_g, w_out):
    heads, chunk, _ = sgu_w[l].shape
    sgu_width = heads * HEAD_DIM
    w_tri = jnp.where(jnp.tril(jnp.ones((chunk, chunk), dtype=bool))[None], sgu_w[l], 0.0)
    w_pair = w_tri.reshape(heads // 2, 2, chunk, chunk).transpose(0, 2, 1, 3).reshape(heads // 2, chunk, 2 * chunk)
    bias = jnp.repeat(sgu_b[l].T, HEAD_DIM, axis=1)
    head_id = jnp.arange(sgu_width) // HEAD_DIM
    mavg = jnp.where(head_id[:, None] == head_id[None, :], 1.0 / HEAD_DIM, 0.0)
    groups, gdim, _ = pool_w[l].shape
    pool_bd = jnp.zeros((groups * gdim, groups * gdim), F32)
    for g in range(groups):
        pool_bd = pool_bd.at[g * gdim:(g + 1) * gdim, g * gdim:(g + 1) * gdim].set(pool_w[l, g])
    return {
        "pre_g": _row(mix_pre_g[l]), "post_g": _row(mix_post_g[l]), "w_in": w_in[l].astype(BF16),
        "sgu_ng": _row(sgu_norm_g[l]), "sgu_nb": _row(sgu_norm_b[l]), "sgu_w": w_pair.astype(BF16),
        "sgu_b": bias, "mavg": mavg.astype(BF16), "conv_w": conv_w[l], "conv_b": _row(conv_b[l]),
        "conv_ng": _row(conv_norm_g[l]), "conv_nb": _row(conv_norm_b[l]), "pool_w": pool_bd.astype(BF16),
        "pool_scale": _row(pool_scale[l]), "branch_g": _row(branch_g[l]), "w_out": w_out[l].astype(BF16),
    }


def _ffn_params(l, ffn_pre_g, ffn_post_g, ffn_up, ffn_conv_w, ffn_conv_b, ffn_down):
    return {
        "pre_g": _row(ffn_pre_g[l]), "post_g": _row(ffn_post_g[l]),
        "up": ffn_up[l].astype(BF16), "cw": ffn_conv_w[l], "cb": _row(ffn_conv_b[l]),
        "down": ffn_down[l].astype(BF16),
    }


def kernel(x, c, mod_w, mod_b, mix_pre_g, mix_post_g, w_in, sgu_norm_g, sgu_norm_b, sgu_w, sgu_b, conv_w, conv_b, conv_norm_g, conv_norm_b, pool_w, pool_scale, branch_g, w_out, ffn_pre_g, ffn_post_g, ffn_up, ffn_conv_w, ffn_conv_b, ffn_down):
    depth = mod_w.shape[0]
    bsz, seq, d = x.shape
    assert seq % SEQ_TILE == 0 and SEQ_TILE % MIXER_SUB_TILE == 0 and SEQ_TILE % FFN_SUB_TILE == 0
    assert MIXER_SUB_TILE % CHUNK == 0 and MIXER_SUB_TILE >= CONV_HALO
    assert ffn_down.shape[1] % FFN_CHUNK == 0
    mod = _modulation(c, mod_w, mod_b).reshape(depth, bsz, N_MOD, d)
    for l in range(depth):
        mp = _mixer_params(l, mix_pre_g, mix_post_g, w_in, sgu_norm_g, sgu_norm_b, sgu_w, sgu_b, conv_w,
                           conv_b, conv_norm_g, conv_norm_b, pool_w, pool_scale, branch_g, w_out)
        fp = _ffn_params(l, ffn_pre_g, ffn_post_g, ffn_up, ffn_conv_w, ffn_conv_b, ffn_down)
        x = _mixer(x, mod[l], mp)
        x = _ffn(x, mod[l], fp)
    return x
```
